```python
import math
import jax, jax.numpy as jnp
from jax import lax
import numpy as np

D_MODEL = 2048
BATCH = 8
SEQ = 4096
DEPTH = 4
DEC_BATCH = 4
DEC_SEQ = 4096
PAST_LEN = 128

GRID_W = 64
Q_BLOCK = 128
HEAD_DIM = 128
ROPE_THETA = 10000.0
EPS = 1e-6
N_EVEN = (DEPTH + 1) // 2
N_ODD = DEPTH // 2
MLA_HEADS = 8
MLA_Q_LORA = 512
MLA_KV_LORA = 512
MLA_NOPE = 128
MLA_ROPE = 64
MLA_V = 128
GQA_HEADS = 8
GQA_KV_HEADS = 2
NA_HEADS = 8
NA_WIN_H = 8
NA_WIN_W = 16
DIFF_HEADS = 4
DIFF_V = 2 * HEAD_DIM
MEM_LEN = 256
MEM_HEADS = 4
D_FF = 4 * D_MODEL

AB_SPLITS = [MLA_Q_LORA, MLA_KV_LORA, MLA_ROPE, GQA_HEADS * HEAD_DIM,
             GQA_KV_HEADS * HEAD_DIM, GQA_KV_HEADS * HEAD_DIM]
CD_SPLITS = [NA_HEADS * HEAD_DIM] * 3 + [DIFF_HEADS * 2 * HEAD_DIM] * 2 + [DIFF_HEADS * DIFF_V]
AB_IN = sum(AB_SPLITS)
CD_IN = sum(CD_SPLITS)
MIX_AB = MLA_HEADS * MLA_V + GQA_HEADS * HEAD_DIM
MIX_CD = NA_HEADS * HEAD_DIM + DIFF_HEADS * DIFF_V

kernel_name = "hybrid_mla_gqa_natten_diff_encoder"


def rms_norm(x, g):
    xf = x.astype(jnp.float32)
    y = xf * lax.rsqrt(jnp.mean(xf * xf, axis=-1, keepdims=True) + EPS)
    return (y * g.astype(jnp.float32)).astype(x.dtype)


def rope_cos_sin(pos, dim):
    inv = ROPE_THETA ** (-jnp.arange(0, dim, 2, dtype=jnp.float32) / dim)
    ang = pos.astype(jnp.float32)[:, None] * inv[None, :]
    return jnp.cos(ang)[None, :, None, :], jnp.sin(ang)[None, :, None, :]


def apply_rope(x, cs):
    cos, sin = cs
    x1, x2 = jnp.split(x.astype(jnp.float32), 2, axis=-1)
    return jnp.concatenate([x1 * cos - x2 * sin, x2 * cos + x1 * sin], axis=-1).astype(x.dtype)


def axial_rope(x, cs_row, cs_col):
    half = x.shape[-1] // 2
    return jnp.concatenate([apply_rope(x[..., :half], cs_row), apply_rope(x[..., half:], cs_col)], axis=-1)


def sweep_query_blocks(fn, *qs):
    b, s = qs[0].shape[:2]
    nb = s // Q_BLOCK
    blocked = tuple(jnp.moveaxis(q.reshape(b, nb, Q_BLOCK, *q.shape[2:]), 1, 0) for q in qs)
    out = lax.map(lambda a: fn(a[0], *a[1:]), (jnp.arange(nb),) + blocked)
    return jnp.moveaxis(out, 0, 1).reshape(b, s, *out.shape[3:])


def dense_attention(q, k, v, scale):
    def block(i, qb):
        s = jnp.einsum('bqhgd,bkhd->bhgqk', qb, k).astype(jnp.float32) * scale
        p = jax.nn.softmax(s, axis=-1).astype(v.dtype)
        return jnp.einsum('bhgqk,bkhd->bqhgd', p, v)
    return sweep_query_blocks(block, q)


def mla_mixer(c_q, c_kv, k_rope, q_norm, w_q_b, kv_norm, w_kv_b, cs_rope):
    b, s, _ = c_q.shape
    q = (rms_norm(c_q, q_norm) @ w_q_b).reshape(b, s, MLA_HEADS, MLA_NOPE + MLA_ROPE)
    q = jnp.concatenate([q[..., :MLA_NOPE], apply_rope(q[..., MLA_NOPE:], cs_rope)], axis=-1)
    kv = (rms_norm(c_kv, kv_norm) @ w_kv_b).reshape(b, s, MLA_HEADS, MLA_NOPE + MLA_V)
    k_r = apply_rope(k_rope[:, :, None, :], cs_rope)
    k = jnp.concatenate([kv[..., :MLA_NOPE], jnp.broadcast_to(k_r, (b, s, MLA_HEADS, MLA_ROPE))], axis=-1)
    o = dense_attention(q[:, :, :, None, :], k, kv[..., MLA_NOPE:], (MLA_NOPE + MLA_ROPE) ** -0.5)
    return o.reshape(b, s, MLA_HEADS * MLA_V)


def gqa_mixer(q, k, v, q_norm, k_norm, cs_row, cs_col):
    b, s, _ = q.shape
    g = GQA_HEADS // GQA_KV_HEADS
    q = axial_rope(rms_norm(q.reshape(b, s, GQA_HEADS, HEAD_DIM), q_norm), cs_row, cs_col)
    k = axial_rope(rms_norm(k.reshape(b, s, GQA_KV_HEADS, HEAD_DIM), k_norm), cs_row, cs_col)
    v = v.reshape(b, s, GQA_KV_HEADS, HEAD_DIM)
    o = dense_attention(q.reshape(b, s, GQA_KV_HEADS, g, HEAD_DIM), k, v, HEAD_DIM ** -0.5)
    return o.reshape(b, s, GQA_HEADS * HEAD_DIM)


def neighbourhood_mixer(q, k, v, rpb):
    b, s, _ = q.shape
    rows = s // GRID_W
    kh = min(NA_WIN_H, rows)
    kw = NA_WIN_W
    rows_per_block = Q_BLOCK // GRID_W
    band = min(kh + rows_per_block - 1, rows)
    q = q.reshape(b, s, NA_HEADS, HEAD_DIM)
    k_grid = k.reshape(b, rows, GRID_W, NA_HEADS, HEAD_DIM)
    v_grid = v.reshape(b, rows, GRID_W, NA_HEADS, HEAD_DIM)
    key_col = jnp.tile(jnp.arange(GRID_W), band)
    key_row_local = jnp.repeat(jnp.arange(band), GRID_W)
    q_local = jnp.arange(Q_BLOCK)

    def block(i, qb):
        r0 = i * rows_per_block
        qr = r0 + q_local // GRID_W
        qc = q_local % GRID_W
        sr = jnp.clip(qr - kh // 2, 0, rows - kh)
        sc = jnp.clip(qc - kw // 2, 0, GRID_W - kw)
        b0 = jnp.clip(r0 - kh // 2, 0, rows - band)
        kb = lax.dynamic_slice_in_dim(k_grid, b0, band, axis=1).reshape(b, band * GRID_W, NA_HEADS, HEAD_DIM)
        vb = lax.dynamic_slice_in_dim(v_grid, b0, band, axis=1).reshape(b, band * GRID_W, NA_HEADS, HEAD_DIM)
        kr = b0 + key_row_local
        inside = ((kr[None, :] >= sr[:, None]) & (kr[None, :] < sr[:, None] + kh)
                  & (key_col[None, :] >= sc[:, None]) & (key_col[None, :] < sc[:, None] + kw))
        dr = jnp.clip(kr[None, :] - qr[:, None] + NA_WIN_H - 1, 0, 2 * NA_WIN_H - 2)
        dc = jnp.clip(key_col[None, :] - qc[:, None] + NA_WIN_W - 1, 0, 2 * NA_WIN_W - 2)
        bias = rpb[:, dr, dc].astype(jnp.float32)
        sco = jnp.einsum('bqhd,bkhd->bhqk', qb, kb).astype(jnp.float32) * HEAD_DIM ** -0.5 + bias[None]
        sco = jnp.where(inside[None, None], sco, -jnp.inf)
        p = jax.nn.softmax(sco, axis=-1).astype(vb.dtype)
        return jnp.einsum('bhqk,bkhd->bqhd', p, vb)

    o = sweep_query_blocks(block, q)
    return o.reshape(b, s, NA_HEADS * HEAD_DIM)


def diff_mixer(q, k, v, lam_p, subln, lambda_init, cs):
    b, s, _ = q.shape
    q = apply_rope(q.reshape(b, s, 2 * DIFF_HEADS, HEAD_DIM), cs).reshape(b, s, DIFF_HEADS, 2, HEAD_DIM)
    k = apply_rope(k.reshape(b, s, 2 * DIFF_HEADS, HEAD_DIM), cs).reshape(b, s, DIFF_HEADS, 2, HEAD_DIM)
    v = v.reshape(b, s, DIFF_HEADS, DIFF_V)
    lp = lam_p.astype(jnp.float32)
    lam = jnp.exp(jnp.sum(lp[0] * lp[1])) - jnp.exp(jnp.sum(lp[2] * lp[3])) + lambda_init

    def block(i, qb):
        sco = jnp.einsum('bqhcd,bkhcd->bchqk', qb, k).astype(jnp.float32) * HEAD_DIM ** -0.5
        p = jax.nn.softmax(sco, axis=-1)
        w = (p[:, 0] - lam * p[:, 1]).astype(v.dtype)
        return jnp.einsum('bhqk,bkhd->bqhd', w, v)

    o = sweep_query_blocks(block, q)
    o = rms_norm(o, subln) * (1.0 - lambda_init)
    return o.reshape(b, s, DIFF_HEADS * DIFF_V)


def memory_cross_attention(h, mem, mem_norm, w_q, w_kv, w_o):
    b, s, _ = h.shape
    m = rms_norm(mem, mem_norm)
    q = (h @ w_q).reshape(b, s, MEM_HEADS, HEAD_DIM)
    kv = (m @ w_kv).reshape(b, mem.shape[1], 2, MEM_HEADS, HEAD_DIM)
    sco = jnp.einsum('bqhd,bkhd->bhqk', q, kv[:, :, 0]).astype(jnp.float32) * HEAD_DIM ** -0.5
    p = jax.nn.softmax(sco, axis=-1).astype(h.dtype)
    o = jnp.einsum('bhqk,bkhd->bqhd', p, kv[:, :, 1]).reshape(b, s, MEM_HEADS * HEAD_DIM)
    return o @ w_o


def run_trunk(x, mem, p):
    b, s, _ = x.shape
    t = jnp.arange(s)
    cs_1d = rope_cos_sin(t, HEAD_DIM)
    cs_mla = rope_cos_sin(t, MLA_ROPE)
    cs_row = rope_cos_sin(t // GRID_W, HEAD_DIM // 2)
    cs_col = rope_cos_sin(t % GRID_W, HEAD_DIM // 2)
    ab_idx = np.cumsum(AB_SPLITS)[:-1].tolist()
    cd_idx = np.cumsum(CD_SPLITS)[:-1].tolist()
    for layer in range(DEPTH):
        j = layer // 2
        h = rms_norm(x, p['norm_mix'][layer])
        if layer % 2 == 0:
            c_q, c_kv, k_rope, gq, gk, gv = jnp.split(h @ p['w_in_ab'][j], ab_idx, axis=-1)
            o_a = mla_mixer(c_q, c_kv, k_rope, p['mla_q_norm'][j], p['w_q_b'][j],
                            p['mla_kv_norm'][j], p['w_kv_b'][j], cs_mla)
            o_b = gqa_mixer(gq, gk, gv, p['gqa_q_norm'][j], p['gqa_k_norm'][j], cs_row, cs_col)
            x = x + jnp.concatenate([o_a, o_b], axis=-1) @ p['w_out_ab'][j]
        else:
            nq, nk, nv, dq, dk, dv = jnp.split(h @ p['w_in_cd'][j], cd_idx, axis=-1)
            lambda_init = 0.8 - 0.6 * math.exp(-0.3 * layer)
            o_c = neighbourhood_mixer(nq, nk, nv, p['na_rpb'][j])
            o_d = diff_mixer(dq, dk, dv, p['diff_lambda'][j], p['diff_subln'][j], lambda_init, cs_1d)
            x = x + jnp.concatenate([o_c, o_d], axis=-1) @ p['w_out_cd'][j]
        h = rms_norm(x, p['norm_xattn'][layer])
        x = x + memory_cross_attention(h, mem, p['norm_mem'][layer], p['w_xq'][layer],
                                       p['w_xkv'][layer], p['w_xo'][layer])
        h = rms_norm(x, p['norm_mlp'][layer])
        x = x + jnp.square(jax.nn.relu(h @ p['w_up'][layer])) @ p['w_down'][layer]
    return rms_norm(x, p['final_norm'])


def setup_inputs(seed: int = 0) -> dict:
    key = jax.random.key(seed)
    k = jax.random.split(key, 27)

    def nrm(i, shape, scale):
        return jax.random.normal(k[i], shape, jnp.float32) * scale

    def gain(i, shape):
        return 1.0 + nrm(i, shape, 0.02)

    D = D_MODEL
    return {
        "x_prompt": nrm(0, (BATCH, SEQ, D), 1.0),
        "x_sample": nrm(1, (DEC_BATCH, DEC_SEQ, D), 1.0),
        "mem_prompt": nrm(2, (BATCH, MEM_LEN, D), 1.0),
        "mem_sample": nrm(3, (DEC_BATCH, MEM_LEN, D), 1.0),
        "norm_mix": gain(4, (DEPTH, D)),
        "w_in_ab": nrm(5, (N_EVEN, D, AB_IN), D ** -0.5),
        "mla_q_norm": gain(6, (N_EVEN, MLA_Q_LORA)),
        "w_q_b": nrm(7, (N_EVEN, MLA_Q_LORA, MLA_HEADS * (MLA_NOPE + MLA_ROPE)), MLA_Q_LORA ** -0.5),
        "mla_kv_norm": gain(8, (N_EVEN, MLA_KV_LORA)),
        "w_kv_b": nrm(9, (N_EVEN, MLA_KV_LORA, MLA_HEADS * (MLA_NOPE + MLA_V)), MLA_KV_LORA ** -0.5),
        "gqa_q_norm": gain(10, (N_EVEN, HEAD_DIM)),
        "gqa_k_norm": gain(11, (N_EVEN, HEAD_DIM)),
        "w_out_ab": nrm(12, (N_EVEN, MIX_AB, D), MIX_AB ** -0.5),
        "w_in_cd": nrm(13, (N_ODD, D, CD_IN), D ** -0.5),
        "na_rpb": nrm(14, (N_ODD, NA_HEADS, 2 * NA_WIN_H - 1, 2 * NA_WIN_W - 1), 0.02),
        "diff_lambda": nrm(15, (N_ODD, 4, HEAD_DIM), 0.1),
        "diff_subln": gain(16, (N_ODD, DIFF_V)),
        "w_out_cd": nrm(17, (N_ODD, MIX_CD, D), MIX_CD ** -0.5),
        "norm_xattn": gain(18, (DEPTH, D)),
        "norm_mem": gain(19, (DEPTH, D)),
        "w_xq": nrm(20, (DEPTH, D, MEM_HEADS * HEAD_DIM), D ** -0.5),
        "w_xkv": nrm(21, (DEPTH, D, 2 * MEM_HEADS * HEAD_DIM), D ** -0.5),
        "w_xo": nrm(22, (DEPTH, MEM_HEADS * HEAD_DIM, D), (MEM_HEADS * HEAD_DIM) ** -0.5),
        "norm_mlp": gain(23, (DEPTH, D)),
        "w_up": nrm(24, (DEPTH, D, D_FF), D ** -0.5),
        "w_down": nrm(25, (DEPTH, D_FF, D), D_FF ** -0.5),
        "final_norm": gain(26, (D,)),
    }


def reference(x_prompt, x_sample, mem_prompt, mem_sample, norm_mix, w_in_ab, mla_q_norm, w_q_b,
              mla_kv_norm, w_kv_b, gqa_q_norm, gqa_k_norm, w_out_ab, w_in_cd, na_rpb, diff_lambda,
              diff_subln, w_out_cd, norm_xattn, norm_mem, w_xq, w_xkv, w_xo, norm_mlp, w_up, w_down,
              final_norm):
    params = dict(norm_mix=norm_mix, w_in_ab=w_in_ab, mla_q_norm=mla_q_norm, w_q_b=w_q_b,
                  mla_kv_norm=mla_kv_norm, w_kv_b=w_kv_b, gqa_q_norm=gqa_q_norm, gqa_k_norm=gqa_k_norm,
                  w_out_ab=w_out_ab, w_in_cd=w_in_cd, na_rpb=na_rpb, diff_lambda=diff_lambda,
                  diff_subln=diff_subln, w_out_cd=w_out_cd, norm_xattn=norm_xattn, norm_mem=norm_mem,
                  w_xq=w_xq, w_xkv=w_xkv, w_xo=w_xo, norm_mlp=norm_mlp, w_up=w_up, w_down=w_down,
                  final_norm=final_norm)
    y_prompt = run_trunk(x_prompt, mem_prompt, params)
    y_sample = run_trunk(x_sample, mem_sample, params)
    return (y_prompt, y_sample)
```

```python
import functools
import math

import numpy as np
import jax
import jax.numpy as jnp
from jax import lax
from jax.experimental import pallas as pl
from jax.experimental.pallas import tpu as pltpu

F32 = jnp.float32
BF16 = jnp.bfloat16

D_MODEL = 2048
DEPTH = 4
GRID_W = 64
HEAD_DIM = 128
ROPE_THETA = 10000.0
EPS = 1e-6
MLA_HEADS = 8
MLA_Q_LORA = 512
MLA_KV_LORA = 512
MLA_NOPE = 128
MLA_ROPE = 64
MLA_V = 128
GQA_HEADS = 8
GQA_KV_HEADS = 2
NA_HEADS = 8
NA_WIN_H = 8
NA_WIN_W = 16
NA_BAND = NA_WIN_H + 1
NA_QBLK = 2 * GRID_W
NA_KBLK = NA_BAND * GRID_W
NA_RPB = (2 * NA_WIN_H - 1) * (2 * NA_WIN_W - 1)
DIFF_HEADS = 4
DIFF_V = 2 * HEAD_DIM
MEM_LEN = 256
MEM_HEADS = 4
D_FF = 4 * D_MODEL
LANES = 128
MASKED = -1e30

AB_CQ, AB_CKV, AB_GQ, AB_GK, AB_GV, AB_KR = 0, 512, 1024, 2048, 2304, 2560
AB_N = 2688
CD_N = 6144

VMEM_LIMIT = 56 * 1024 * 1024


def _params(*sem):
    return pltpu.CompilerParams(dimension_semantics=sem, vmem_limit_bytes=VMEM_LIMIT)


def _rms(xf, gain):
    ms = jnp.mean(xf * xf, axis=-1, keepdims=True)
    return xf * lax.rsqrt(ms + EPS) * gain


_NT = (((1,), (1,)), ((), ()))


def _norm_matmul_kernel(x_ref, g_ref, w_ref, o_ref, hn_ref):
    @pl.when(pl.program_id(1) == 0)
    def _():
        hn_ref[...] = _rms(x_ref[...].astype(F32), g_ref[...]).astype(BF16)

    o_ref[...] = jnp.dot(hn_ref[...], w_ref[...], preferred_element_type=F32).astype(o_ref.dtype)


def _norm_matmul(x, gain, w, *, k, x_col, tm, tn, name):
    t, n = x.shape[0], w.shape[1]
    return pl.pallas_call(
        _norm_matmul_kernel,
        out_shape=jax.ShapeDtypeStruct((t, n), BF16),
        grid=(t // tm, n // tn),
        in_specs=[pl.BlockSpec((tm, k), lambda i, j: (i, x_col)),
                  pl.BlockSpec((1, k), lambda i, j: (0, 0)),
                  pl.BlockSpec((k, tn), lambda i, j: (0, j))],
        out_specs=pl.BlockSpec((tm, tn), lambda i, j: (i, j)),
        scratch_shapes=[pltpu.VMEM((tm, k), BF16)],
        compiler_params=_params("parallel", "arbitrary"),
        name=name,
    )(x, gain, w)


def _swap_halves(v, half):
    if 2 * half == LANES:
        return pltpu.roll(v, half, 1)
    lane = lax.broadcasted_iota(jnp.int32, v.shape, 1)
    first = (lane & (2 * half - 1)) < half
    return jnp.where(first, pltpu.roll(v, LANES - half, 1), pltpu.roll(v, half, 1))


def _prep_kernel(*refs, n_src, n_gain, plan):
    src = refs[:n_src]
    cos_ref, sin_ref = refs[n_src], refs[n_src + 1]
    gains = refs[n_src + 2:n_src + 2 + n_gain]
    o_ref = refs[-1]
    done = {}
    for og, step in enumerate(plan):
        if step not in done:
            si, g, gi, half, scale = step
            v = src[si][:, g * LANES:(g + 1) * LANES].astype(F32)
            if gi is not None:
                v = _rms(v, gains[gi][...])
            if half:
                v = v * cos_ref[...] + _swap_halves(v, half) * sin_ref[...]
            if scale != 1.0:
                v = v * scale
            done[step] = v.astype(BF16)
        o_ref[:, og * LANES:(og + 1) * LANES] = done[step]


def _prep(srcs, plan, cos, sin, gains, *, s, tm, name):
    t = srcs[0][0].shape[0]
    ns = s // tm
    n_out = len(plan) * LANES
    in_specs = [pl.BlockSpec((tm, w), functools.partial(lambda i, c: (i, c), c=c)) for _, w, c in srcs]
    in_specs += [pl.BlockSpec((tm, LANES), lambda i: (i % ns, 0))] * 2
    in_specs += [pl.BlockSpec((1, LANES), lambda i: (0, 0))] * len(gains)
    return pl.pallas_call(
        functools.partial(_prep_kernel, n_src=len(srcs), n_gain=len(gains), plan=tuple(plan)),
        out_shape=jax.ShapeDtypeStruct((t, n_out), BF16),
        grid=(t // tm,),
        in_specs=in_specs,
        out_specs=pl.BlockSpec((tm, n_out), lambda i: (i, 0)),
        compiler_params=_params("parallel"),
        name=name,
    )(*[a for a, _, _ in srcs], cos, sin, *gains)


def _attn_kernel(q_ref, k_ref, v_ref, o_ref):
    s = lax.dot_general(q_ref[...], k_ref[...], _NT, preferred_element_type=F32)
    p = jnp.exp(s - jnp.max(s, axis=-1, keepdims=True))
    l = jnp.sum(p, axis=-1, keepdims=True)
    o = jnp.dot(p.astype(BF16), v_ref[...], preferred_element_type=F32)
    o_ref[...] = (o / l).astype(o_ref.dtype)


def _attention(q, k, v, *, b, s, heads, dq, dv, qcol, kcol, vcol, tq, name):
    nq = s // tq
    return pl.pallas_call(
        _attn_kernel,
        out_shape=jax.ShapeDtypeStruct((b * s, heads * dv), BF16),
        grid=(b, heads, nq),
        in_specs=[pl.BlockSpec((tq, dq), lambda bi, h, i: (bi * nq + i, qcol(h))),
                  pl.BlockSpec((s, dq), lambda bi, h, i: (bi, kcol(h))),
                  pl.BlockSpec((s, dv), lambda bi, h, i: (bi, vcol(h)))],
        out_specs=pl.BlockSpec((tq, dv), lambda bi, h, i: (bi * nq + i, h)),
        compiler_params=_params("parallel", "parallel", "arbitrary"),
        name=name,
    )(q, k, v)


def _diff_attn_kernel(q_ref, k_ref, v_ref, lam_ref, sub_ref, o_ref, *, lambda_init):
    q, k = q_ref[...], k_ref[...]
    probs = []
    for c in range(2):
        s = lax.dot_general(q[:, c * HEAD_DIM:(c + 1) * HEAD_DIM], k[:, c * HEAD_DIM:(c + 1) * HEAD_DIM],
                            _NT, preferred_element_type=F32)
        p = jnp.exp(s - jnp.max(s, axis=-1, keepdims=True))
        probs.append(p * (1.0 / jnp.sum(p, axis=-1, keepdims=True)))
    lp = lam_ref[...]
    lam = (jnp.exp(jnp.sum(lp[0:1] * lp[1:2], axis=-1, keepdims=True))
           - jnp.exp(jnp.sum(lp[2:3] * lp[3:4], axis=-1, keepdims=True)) + lambda_init)
    w = (probs[0] - lam * probs[1]).astype(BF16)
    o = jnp.dot(w, v_ref[...], preferred_element_type=F32)
    o_ref[...] = (_rms(o, sub_ref[...]) * (1.0 - lambda_init)).astype(o_ref.dtype)


def _diff_attention(q, k, proj, lam_p, subln, *, b, s, lambda_init, tq, name):
    nq = s // tq
    vcol0 = 5 * DIFF_HEADS
    return pl.pallas_call(
        functools.partial(_diff_attn_kernel, lambda_init=lambda_init),
        out_shape=jax.ShapeDtypeStruct((b * s, DIFF_HEADS * DIFF_V), BF16),
        grid=(b, DIFF_HEADS, nq),
        in_specs=[pl.BlockSpec((tq, 2 * HEAD_DIM), lambda bi, h, i: (bi * nq + i, h)),
                  pl.BlockSpec((s, 2 * HEAD_DIM), lambda bi, h, i: (bi, h)),
                  pl.BlockSpec((s, DIFF_V), lambda bi, h, i: (bi, vcol0 + h)),
                  pl.BlockSpec((4, HEAD_DIM), lambda bi, h, i: (0, 0)),
                  pl.BlockSpec((1, DIFF_V), lambda bi, h, i: (0, 0))],
        out_specs=pl.BlockSpec((tq, DIFF_V), lambda bi, h, i: (bi * nq + i, h)),
        compiler_params=_params("parallel", "parallel", "arbitrary"),
        name=name,
    )(q, k, proj, lam_p, subln)


def _na_geometry(s):
    rows = s // GRID_W
    assert s % NA_QBLK == 0 and rows >= NA_BAND
    classes, cls = [], []
    for i in range(rows // 2):
        r0 = 2 * i
        b0 = min(max(r0 - NA_WIN_H // 2, 0), rows - NA_BAND)
        sr = [min(max(r0 + t - NA_WIN_H // 2, 0), rows - NA_WIN_H) for t in (0, 1)]
        key = (b0 - r0, sr[0] - b0, sr[1] - b0)
        if key not in classes:
            classes.append(key)
        cls.append(classes.index(key))
    return tuple(classes), np.asarray(cls, np.int32)


def _na_bias_kernel(rpb_ref, o_ref, *, classes):
    h = pl.program_id(0)
    qi = lax.broadcasted_iota(jnp.int32, (NA_QBLK, NA_KBLK), 0)
    ki = lax.broadcasted_iota(jnp.int32, (NA_QBLK, NA_KBLK), 1)
    qrl, qc = qi >> 6, qi & (GRID_W - 1)
    krl, kc = ki >> 6, ki & (GRID_W - 1)
    dc = jnp.clip(kc - qc + (NA_WIN_W - 1), 0, 2 * NA_WIN_W - 2)
    sc = jnp.clip(qc - NA_WIN_W // 2, 0, GRID_W - NA_WIN_W)
    col_ok = (kc >= sc) & (kc < sc + NA_WIN_W)
    for c, (d, s0, s1) in enumerate(classes):
        dr = jnp.clip(krl - qrl + (d + NA_WIN_H - 1), 0, 2 * NA_WIN_H - 2)
        idx = dr * (2 * NA_WIN_W - 1) + dc
        srl = jnp.where(qrl == 0, s0, s1)
        inside = col_ok & (krl >= srl) & (krl < srl + NA_WIN_H)
        bias = lax.fori_loop(0, NA_RPB, lambda t, acc: jnp.where(idx == t, rpb_ref[h, t], acc),
                             jnp.zeros((NA_QBLK, NA_KBLK), F32))
        o_ref[0, c] = jnp.where(inside, bias, MASKED)


def _na_bias(rpb, classes):
    return pl.pallas_call(
        functools.partial(_na_bias_kernel, classes=classes),
        out_shape=jax.ShapeDtypeStruct((NA_HEADS, len(classes), NA_QBLK, NA_KBLK), F32),
        grid=(NA_HEADS,),
        in_specs=[pl.BlockSpec(memory_space=pltpu.SMEM)],
        out_specs=pl.BlockSpec((1, len(classes), NA_QBLK, NA_KBLK), lambda h: (h, 0, 0, 0)),
        compiler_params=_params("parallel"),
        name="na_bias",
    )(rpb.reshape(NA_HEADS, NA_RPB))


def _na_attn_kernel(cls_ref, q_ref, k_ref, v_ref, bias_ref, o_ref, *, rows):
    scale = HEAD_DIM ** -0.5

    def block(i, carry):
        b0 = jnp.clip(2 * i - NA_WIN_H // 2, 0, rows - NA_BAND)
        q0 = pl.multiple_of(i * NA_QBLK, NA_QBLK)
        k0 = pl.multiple_of(b0 * GRID_W, GRID_W)
        q = (q_ref[pl.ds(q0, NA_QBLK), :].astype(F32) * scale).astype(BF16)
        s = lax.dot_general(q, k_ref[pl.ds(k0, NA_KBLK), :], _NT, preferred_element_type=F32)
        s = s + bias_ref[0, cls_ref[i]]
        p = jnp.exp(s - jnp.max(s, axis=-1, keepdims=True))
        l = jnp.sum(p, axis=-1, keepdims=True)
        o = jnp.dot(p.astype(BF16), v_ref[pl.ds(k0, NA_KBLK), :], preferred_element_type=F32)
        o_ref[pl.ds(q0, NA_QBLK), :] = (o / l).astype(o_ref.dtype)
        return carry

    lax.fori_loop(0, rows // 2, block, 0)


def _na_attention(proj, bias, cls, *, b, s, name):
    rows = s // GRID_W
    nc = bias.shape[1]
    return pl.pallas_call(
        functools.partial(_na_attn_kernel, rows=rows),
        out_shape=jax.ShapeDtypeStruct((b * s, NA_HEADS * HEAD_DIM), BF16),
        grid=(b, NA_HEADS),
        in_specs=[pl.BlockSpec(memory_space=pltpu.SMEM),
                  pl.BlockSpec((s, HEAD_DIM), lambda bi, h: (bi, h)),
                  pl.BlockSpec((s, HEAD_DIM), lambda bi, h: (bi, NA_HEADS + h)),
                  pl.BlockSpec((s, HEAD_DIM), lambda bi, h: (bi, 2 * NA_HEADS + h)),
                  pl.BlockSpec((1, nc, NA_QBLK, NA_KBLK), lambda bi, h: (h, 0, 0, 0))],
        out_specs=pl.BlockSpec((s, HEAD_DIM), lambda bi, h: (bi, h)),
        compiler_params=_params("parallel", "arbitrary"),
        name=name,
    )(cls, proj, proj, proj, bias)


def _out_proj_kernel(x_ref, a_ref, b_ref, w_ref, o_ref):
    ab = jnp.concatenate([a_ref[...], b_ref[...]], axis=1)
    o_ref[...] = x_ref[...] + jnp.dot(ab, w_ref[...], preferred_element_type=F32)


def _out_proj(x, a, bb, w, *, tm, tn, name):
    t, n = x.shape
    ka, kb = a.shape[1], bb.shape[1]
    return pl.pallas_call(
        _out_proj_kernel,
        out_shape=jax.ShapeDtypeStruct((t, n), F32),
        grid=(t // tm, n // tn),
        in_specs=[pl.BlockSpec((tm, tn), lambda i, j: (i, j)),
                  pl.BlockSpec((tm, ka), lambda i, j: (i, 0)),
                  pl.BlockSpec((tm, kb), lambda i, j: (i, 0)),
                  pl.BlockSpec((ka + kb, tn), lambda i, j: (0, j))],
        out_specs=pl.BlockSpec((tm, tn), lambda i, j: (i, j)),
        compiler_params=_params("parallel", "arbitrary"),
        name=name,
    )(x, a, bb, w)


def _xattn_kernel(x_ref, g_ref, wq_ref, kv_ref, wo_ref, o_ref):
    xf = x_ref[...]
    hn = _rms(xf, g_ref[...]).astype(BF16)
    q = (jnp.dot(hn, wq_ref[...], preferred_element_type=F32) * HEAD_DIM ** -0.5).astype(BF16)
    kv = kv_ref[...]
    heads = []
    for h in range(MEM_HEADS):
        lo, hi = h * HEAD_DIM, (h + 1) * HEAD_DIM
        s = lax.dot_general(q[:, lo:hi], kv[:, lo:hi], _NT, preferred_element_type=F32)
        p = jnp.exp(s - jnp.max(s, axis=-1, keepdims=True))
        l = jnp.sum(p, axis=-1, keepdims=True)
        o = jnp.dot(p.astype(BF16), kv[:, MEM_HEADS * HEAD_DIM + lo:MEM_HEADS * HEAD_DIM + hi],
                    preferred_element_type=F32)
        heads.append((o / l).astype(BF16))
    o_ref[...] = xf + jnp.dot(jnp.concatenate(heads, axis=1), wo_ref[...], preferred_element_type=F32)


def _xattn(x, gain, wq, kv, wo, *, s, tm, name):
    t = x.shape[0]
    per_seq = s // tm
    dq = MEM_HEADS * HEAD_DIM
    return pl.pallas_call(
        _xattn_kernel,
        out_shape=jax.ShapeDtypeStruct((t, D_MODEL), F32),
        grid=(t // tm,),
        in_specs=[pl.BlockSpec((tm, D_MODEL), lambda i: (i, 0)),
                  pl.BlockSpec((1, D_MODEL), lambda i: (0, 0)),
                  pl.BlockSpec((D_MODEL, dq), lambda i: (0, 0)),
                  pl.BlockSpec((MEM_LEN, 2 * dq), lambda i: (i // per_seq, 0)),
                  pl.BlockSpec((dq, D_MODEL), lambda i: (0, 0))],
        out_specs=pl.BlockSpec((tm, D_MODEL), lambda i: (i, 0)),
        compiler_params=_params("parallel"),
        name=name,
    )(x, gain, wq, kv, wo)


def _mlp_kernel(x_ref, g_ref, wu_ref, wd_ref, gf_ref, o_ref, hn_ref, acc_ref, *, final_norm):
    f = pl.program_id(1)

    @pl.when(f == 0)
    def _():
        hn_ref[...] = _rms(x_ref[...], g_ref[...]).astype(BF16)
        acc_ref[...] = jnp.zeros_like(acc_ref)

    u = jnp.dot(hn_ref[...], wu_ref[...], preferred_element_type=F32)
    a = jnp.square(jnp.maximum(u, 0.0)).astype(BF16)
    acc_ref[...] += jnp.dot(a, wd_ref[...], preferred_element_type=F32)

    @pl.when(f == pl.num_programs(1) - 1)
    def _():
        y = x_ref[...] + acc_ref[...]
        if final_norm:
            y = _rms(y, gf_ref[...])
        o_ref[...] = y


def _mlp(x, gain, wu, wd, gf, *, final_norm, tm, tf, name):
    t = x.shape[0]
    return pl.pallas_call(
        functools.partial(_mlp_kernel, final_norm=final_norm),
        out_shape=jax.ShapeDtypeStruct((t, D_MODEL), F32),
        grid=(t // tm, D_FF // tf),
        in_specs=[pl.BlockSpec((tm, D_MODEL), lambda i, f: (i, 0)),
                  pl.BlockSpec((1, D_MODEL), lambda i, f: (0, 0)),
                  pl.BlockSpec((D_MODEL, tf), lambda i, f: (0, f)),
                  pl.BlockSpec((tf, D_MODEL), lambda i, f: (f, 0)),
                  pl.BlockSpec((1, D_MODEL), lambda i, f: (0, 0))],
        out_specs=pl.BlockSpec((tm, D_MODEL), lambda i, f: (i, 0)),
        scratch_shapes=[pltpu.VMEM((tm, D_MODEL), BF16), pltpu.VMEM((tm, D_MODEL), F32)],
        compiler_params=_params("parallel", "arbitrary"),
        name=name,
    )(x, gain, wu, wd, gf)


def _rope_table(pos, dim):
    inv = ROPE_THETA ** (-jnp.arange(0, dim, 2, dtype=F32) / dim)
    ang = pos.astype(F32)[:, None] * inv[None, :]
    return jnp.cos(ang), jnp.sin(ang)


def _tables(s):
    t = jnp.arange(s)
    c1, s1 = _rope_table(t, HEAD_DIM)
    cm, sm = _rope_table(t, MLA_ROPE)
    cr, sr = _rope_table(t // GRID_W, HEAD_DIM // 2)
    cc, sc = _rope_table(t % GRID_W, HEAD_DIM // 2)
    cat = lambda *xs: jnp.concatenate(xs, axis=1)
    return {
        "diff": (cat(c1, c1), cat(-s1, s1)),
        "mla": (cat(cm, cm, cm, cm), cat(-sm, sm, -sm, sm)),
        "axial": (cat(cr, cr, cc, cc), cat(-sr, sr, -sc, sc)),
    }


def _prepare(p):
    row = lambda g: g.astype(F32)[:, None, :]
    cq, ckv, kr, gq, gk, gv = jnp.split(
        p["w_in_ab"], np.cumsum([MLA_Q_LORA, MLA_KV_LORA, MLA_ROPE, GQA_HEADS * HEAD_DIM,
                                 GQA_KV_HEADS * HEAD_DIM])[:].tolist(), axis=-1)
    pad = jnp.zeros(kr.shape[:-1] + (AB_N - AB_KR - MLA_ROPE,), kr.dtype)
    w_in_ab = jnp.concatenate([cq, ckv, gq, gk, gv, kr, pad], axis=-1)
    n_even = p["w_q_b"].shape[0]
    w_q_b = p["w_q_b"].reshape(n_even, MLA_Q_LORA, MLA_HEADS, MLA_NOPE + MLA_ROPE)
    w_q_b = jnp.pad(w_q_b, ((0, 0), (0, 0), (0, 0), (0, 2 * LANES - MLA_NOPE - MLA_ROPE)))
    w_q_b = w_q_b.reshape(n_even, MLA_Q_LORA, MLA_HEADS * 2 * LANES)
    out = {k: p[k].astype(BF16) for k in ("w_kv_b", "w_out_ab", "w_in_cd", "w_out_cd", "w_xq", "w_xkv",
                                          "w_xo", "w_up", "w_down")}
    out["w_in_ab"] = w_in_ab.astype(BF16)
    out["w_q_b"] = w_q_b.astype(BF16)
    for k in ("norm_mix", "mla_q_norm", "mla_kv_norm", "gqa_q_norm", "gqa_k_norm", "diff_subln", "norm_xattn",
              "norm_mem", "norm_mlp"):
        out[k] = row(p[k])
    out["final_norm"] = p["final_norm"].astype(F32)[None, :]
    out["na_rpb"] = p["na_rpb"].astype(F32)
    out["diff_lambda"] = p["diff_lambda"].astype(F32)
    return out


def _even_mixer(x, p, j, layer, tabs, b, s):
    proj = _norm_matmul(x, p["norm_mix"][layer], p["w_in_ab"][j], k=D_MODEL, x_col=0, tm=1024, tn=896,
                        name="in_proj_ab")
    q = _norm_matmul(proj, p["mla_q_norm"][j], p["w_q_b"][j], k=MLA_Q_LORA, x_col=AB_CQ // MLA_Q_LORA,
                     tm=1024, tn=2048, name="mla_q_up")
    kv = _norm_matmul(proj, p["mla_kv_norm"][j], p["w_kv_b"][j], k=MLA_KV_LORA, x_col=AB_CKV // MLA_KV_LORA,
                      tm=1024, tn=2048, name="mla_kv_up")
    cos, sin = tabs["mla"]
    mla_scale = (MLA_NOPE + MLA_ROPE) ** -0.5
    q_plan = [step for h in range(MLA_HEADS)
              for step in ((0, 2 * h, None, 0, mla_scale), (0, 2 * h + 1, None, MLA_ROPE // 2, mla_scale))]
    q = _prep([(q, 2 * LANES * MLA_HEADS, 0)], q_plan, cos, sin, [], s=s, tm=512, name="mla_q_rope")
    k_plan = [step for h in range(MLA_HEADS)
              for step in ((0, 2 * h, None, 0, 1.0), (1, 0, None, MLA_ROPE // 2, 1.0))]
    k = _prep([(kv, 2 * LANES * MLA_HEADS, 0), (proj, LANES, AB_KR // LANES)], k_plan, cos, sin, [],
              s=s, tm=512, name="mla_k_rope")
    o_a = _attention(q, k, kv, b=b, s=s, heads=MLA_HEADS, dq=2 * LANES, dv=MLA_V,
                     qcol=lambda h: h, kcol=lambda h: h, vcol=lambda h: 2 * h + 1, tq=512, name="mla_attn")
    cos, sin = tabs["axial"]
    gq = _prep([(proj, GQA_HEADS * HEAD_DIM, AB_GQ // (GQA_HEADS * HEAD_DIM))],
               [(0, g, 0, HEAD_DIM // 4, HEAD_DIM ** -0.5) for g in range(GQA_HEADS)],
               cos, sin, [p["gqa_q_norm"][j]], s=s, tm=512, name="gqa_q_rope")
    gk = _prep([(proj, GQA_KV_HEADS * HEAD_DIM, AB_GK // (GQA_KV_HEADS * HEAD_DIM))],
               [(0, g, 0, HEAD_DIM // 4, 1.0) for g in range(GQA_KV_HEADS)],
               cos, sin, [p["gqa_k_norm"][j]], s=s, tm=512, name="gqa_k_rope")
    grp = GQA_HEADS // GQA_KV_HEADS
    o_b = _attention(gq, gk, proj, b=b, s=s, heads=GQA_HEADS, dq=HEAD_DIM, dv=HEAD_DIM,
                     qcol=lambda h: h, kcol=lambda h: h // grp, vcol=lambda h: AB_GV // HEAD_DIM + h // grp,
                     tq=512, name="gqa_attn")
    return _out_proj(x, o_a, o_b, p["w_out_ab"][j], tm=512, tn=1024, name="out_proj_ab")


def _odd_mixer(x, p, j, layer, tabs, b, s):
    proj = _norm_matmul(x, p["norm_mix"][layer], p["w_in_cd"][j], k=D_MODEL, x_col=0, tm=1024, tn=1024,
                        name="in_proj_cd")
    classes, cls = _na_geometry(s)
    bias = _na_bias(p["na_rpb"][j], classes)
    o_c = _na_attention(proj, bias, jnp.asarray(cls), b=b, s=s, name="na_attn")
    cos, sin = tabs["diff"]
    width = 2 * DIFF_HEADS * HEAD_DIM
    dq = _prep([(proj, width, 3)], [(0, g, None, HEAD_DIM // 2, HEAD_DIM ** -0.5) for g in range(2 * DIFF_HEADS)],
               cos, sin, [], s=s, tm=512, name="diff_q_rope")
    dk = _prep([(proj, width, 4)], [(0, g, None, HEAD_DIM // 2, 1.0) for g in range(2 * DIFF_HEADS)],
               cos, sin, [], s=s, tm=512, name="diff_k_rope")
    lambda_init = 0.8 - 0.6 * math.exp(-0.3 * layer)
    o_d = _diff_attention(dq, dk, proj, p["diff_lambda"][j], p["diff_subln"][j], b=b, s=s,
                          lambda_init=lambda_init, tq=256, name="diff_attn")
    return _out_proj(x, o_c, o_d, p["w_out_cd"][j], tm=512, tn=1024, name="out_proj_cd")


def _trunk(x, mem, p):
    b, s, _ = x.shape
    tabs = _tables(s)
    x = x.reshape(b * s, D_MODEL)
    mem = mem.reshape(b * MEM_LEN, D_MODEL)
    for layer in range(DEPTH):
        j = layer // 2
        mixer = _even_mixer if layer % 2 == 0 else _odd_mixer
        x = mixer(x, p, j, layer, tabs, b, s)
        kv = _norm_matmul(mem, p["norm_mem"][layer], p["w_xkv"][layer], k=D_MODEL, x_col=0, tm=256, tn=1024,
                          name="mem_kv")
        x = _xattn(x, p["norm_xattn"][layer], p["w_xq"][layer], kv, p["w_xo"][layer], s=s, tm=512, name="xattn")
        x = _mlp(x, p["norm_mlp"][layer], p["w_up"][layer], p["w_down"][layer], p["final_norm"],
                 final_norm=layer == DEPTH - 1, tm=512, tf=512, name="mlp")
    return x.reshape(b, s, D_MODEL)


def kernel(x_prompt, x_sample, mem_prompt, mem_sample, norm_mix, w_in_ab, mla_q_norm, w_q_b, mla_kv_norm, w_kv_b, gqa_q_norm, gqa_k_norm, w_out_ab, w_in_cd, na_rpb, diff_lambda, diff_subln, w_out_cd, norm_xattn, norm_mem, w_xq, w_xkv, w_xo, norm_mlp, w_up, w_down, final_norm):
    p = _prepare(dict(norm_mix=norm_mix, w_in_ab=w_in_ab, mla_q_norm=mla_q_norm, w_q_b=w_q_b,
                      mla_kv_norm=mla_kv_norm, w_kv_b=w_kv_b, gqa_q_norm=gqa_q_norm, gqa_k_norm=gqa_k_norm,
                      w_out_ab=w_out_ab, w_in_cd=w_in_cd, na_rpb=na_rpb, diff_lambda=diff_lambda,
                      diff_subln=diff_subln, w_out_cd=w_out_cd, norm_xattn=norm_xattn, norm_mem=norm_mem,
                      w_xq=w_xq, w_xkv=w_xkv, w_xo=w_xo, norm_mlp=norm_mlp, w_up=w_up, w_down=w_down,
                      final_norm=final_norm))
    nb = x_prompt.shape[0]
    y = _trunk(jnp.concatenate([x_prompt, x_sample], axis=0),
               jnp.concatenate([mem_prompt, mem_sample], axis=0), p)
    return (y[:nb], y[nb:])
```
